```python
import math, functools
import jax, jax.numpy as jnp
from jax import lax
import numpy as np

D_MODEL = 2048
BATCH = 2
SEQ = 16384
DEPTH = 2
DEC_BATCH = 16
DEC_SEQ = 16
PAST_LEN = 4096

CHUNK = 64
N_A = DEPTH // 2
N_B = DEPTH - N_A
POOL_WINDOWS = (2, 4, 8, 16)
N_POOL_GROUPS = len(POOL_WINDOWS)
POOL_GROUP = D_MODEL // N_POOL_GROUPS
POOL_HIST = max(POOL_WINDOWS) - 1
N_HEADS = 8
HEAD_DIM = 128
QK_WIDTH = N_HEADS * 2 * HEAD_DIM
V_HEAD = 2 * HEAD_DIM
V_WIDTH = N_HEADS * V_HEAD
D_FF = 5632
CONV_W = 3
ROPE_THETA = 10000.0
EPS = 1e-5
Q_BLOCK = 128

kernel_name = "yoco_pool_diffattn_streaming_encoder_step"


def _rmsnorm(x, g):
    x32 = x.astype(jnp.float32)
    y = x32 * lax.rsqrt(jnp.mean(x32 * x32, axis=-1, keepdims=True) + EPS)
    return (y * g.astype(jnp.float32)).astype(x.dtype)


def _rope(x, pos):
    half = HEAD_DIM // 2
    inv = jnp.exp(-math.log(ROPE_THETA) * jnp.arange(half, dtype=jnp.float32) / half)
    ang = pos.astype(jnp.float32)[:, None] * inv[None, :]
    cos = jnp.cos(ang)[None, :, None, None, :]
    sin = jnp.sin(ang)[None, :, None, None, :]
    x32 = x.astype(jnp.float32)
    x1, x2 = x32[..., :half], x32[..., half:]
    return jnp.concatenate([x1 * cos - x2 * sin, x2 * cos + x1 * sin], axis=-1).astype(x.dtype)


def _pool_mixer(h, hist, pos, w_g, scale):
    T = h.shape[1]
    xx = jnp.concatenate([hist.astype(h.dtype), h], axis=1)
    x32 = xx.astype(jnp.float32)
    cs = jnp.concatenate([jnp.zeros_like(x32[:, :1]), jnp.cumsum(x32, axis=1)], axis=1)
    end = cs[:, POOL_HIST + 1:]
    cur = x32[:, POOL_HIST:]
    outs = []
    for g, w in enumerate(POOL_WINDOWS):
        sl = slice(g * POOL_GROUP, (g + 1) * POOL_GROUP)
        start = cs[:, POOL_HIST + 1 - w: POOL_HIST + 1 - w + T, sl]
        cnt = jnp.minimum(pos + 1, w).astype(jnp.float32)[None, :, None]
        d = ((end[..., sl] - start) / cnt - cur[..., sl]).astype(h.dtype)
        outs.append(jnp.einsum('btc,cd->btd', d, w_g[g]))
    out = jnp.concatenate(outs, axis=-1) * scale
    return out, xx[:, -POOL_HIST:]


def _conv_ffn(h, hist, w_gate, w_up, conv_w, conv_b, w_down):
    T = h.shape[1]
    g = h @ w_gate
    gg = jnp.concatenate([hist.astype(g.dtype), g], axis=1)
    c = conv_b + gg[:, 0:T] * conv_w[0] + gg[:, 1:1 + T] * conv_w[1] + gg[:, 2:2 + T] * conv_w[2]
    a = jax.nn.silu(c) * (h @ w_up)
    return a @ w_down, gg[:, -(CONV_W - 1):]


def _shared_kv(x, pos, norm_kv, w_k, w_v):
    b, T, _ = x.shape
    hk = _rmsnorm(x, norm_kv)
    k = _rope((hk @ w_k).reshape(b, T, N_HEADS, 2, HEAD_DIM), pos)
    v = (hk @ w_v).reshape(b, T, N_HEADS, V_HEAD)
    return k, v


def _diff_attn_core(q, qpos, k, v, kpos, lam):
    s = jnp.einsum('bqhcd,bkhcd->cbhqk', q, k, preferred_element_type=jnp.float32) * (HEAD_DIM ** -0.5)
    mask = (kpos[None, :] // CHUNK) <= (qpos[:, None] // CHUNK)
    s = jnp.where(mask, s, -jnp.inf)
    p = jax.nn.softmax(s, axis=-1)
    wts = (p[0] - lam * p[1]).astype(v.dtype)
    return jnp.einsum('bhqk,bkhe->bqhe', wts, v)


def _diff_attn(h, pos, k, v, kpos, w_q, lq1, lk1, lq2, lk2, subln, w_o, lam_init):
    b, T, _ = h.shape
    q = _rope((h @ w_q).reshape(b, T, N_HEADS, 2, HEAD_DIM), pos)
    lam = (jnp.exp(jnp.sum(lq1.astype(jnp.float32) * lk1.astype(jnp.float32)))
           - jnp.exp(jnp.sum(lq2.astype(jnp.float32) * lk2.astype(jnp.float32))) + lam_init)
    if T > Q_BLOCK and T % Q_BLOCK == 0:
        nb = T // Q_BLOCK
        qb = jnp.moveaxis(q.reshape(b, nb, Q_BLOCK, N_HEADS, 2, HEAD_DIM), 1, 0)
        pb = pos.reshape(nb, Q_BLOCK)
        o = lax.map(lambda a: _diff_attn_core(a[0], a[1], k, v, kpos, lam), (qb, pb))
        o = jnp.moveaxis(o, 0, 1).reshape(b, T, N_HEADS, V_HEAD)
    else:
        o = _diff_attn_core(q, pos, k, v, kpos, lam)
    o = _rmsnorm(o, subln) * (1.0 - lam_init)
    return o.reshape(b, T, V_WIDTH) @ w_o


def _trunk(x, pos, pool_hist, ffn_hist, past_k, past_v, norm_mix, pool_w, pool_scale, norm_kv,
           w_k, w_v, w_q, lam_q1, lam_k1, lam_q2, lam_k2, subln, w_o, norm_ffn, w_gate, w_up,
           conv_w, conv_b, w_down, norm_final):
    b, T, _ = x.shape
    pool_states, ffn_states = [], []
    k_new = v_new = k_all = v_all = kpos = None
    for l in range(DEPTH):
        h = _rmsnorm(x, norm_mix[l])
        if l < N_A:
            mix, st = _pool_mixer(h, pool_hist[l], pos, pool_w[l], pool_scale[l])
            pool_states.append(st)
        else:
            if l == N_A:
                k_new, v_new = _shared_kv(x, pos, norm_kv, w_k, w_v)
                if past_k is None:
                    k_all, v_all, kpos = k_new, v_new, pos
                else:
                    P = past_k.shape[1]
                    k_all = jnp.concatenate([past_k.reshape(b, P, N_HEADS, 2, HEAD_DIM).astype(k_new.dtype), k_new], axis=1)
                    v_all = jnp.concatenate([past_v.astype(v_new.dtype), v_new], axis=1)
                    kpos = jnp.arange(P + T, dtype=jnp.int32)
            j = l - N_A
            lam_init = 0.8 - 0.6 * math.exp(-0.3 * l)
            mix = _diff_attn(h, pos, k_all, v_all, kpos, w_q[j], lam_q1[j], lam_k1[j], lam_q2[j],
                             lam_k2[j], subln[j], w_o[j], lam_init)
        x = x + mix
        h2 = _rmsnorm(x, norm_ffn[l])
        f, st = _conv_ffn(h2, ffn_hist[l], w_gate[l], w_up[l], conv_w[l], conv_b[l], w_down[l])
        ffn_states.append(st)
        x = x + f
    y = _rmsnorm(x, norm_final)
    return (y, jnp.stack(pool_states), jnp.stack(ffn_states),
            k_new.reshape(b, T, N_HEADS, 2 * HEAD_DIM), v_new)


def setup_inputs(seed: int = 0) -> dict:
    key = jax.random.key(seed)
    ks = jax.random.split(key, 30)

    def nrm(k, shape, scale=1.0):
        return jax.random.normal(k, shape, jnp.float32) * scale

    return {
        "x_prompt": nrm(ks[0], (BATCH, SEQ, D_MODEL)),
        "x_sample": nrm(ks[1], (DEC_BATCH, DEC_SEQ, D_MODEL)),
        "cache_pool": nrm(ks[2], (N_A, DEC_BATCH, POOL_HIST, D_MODEL)),
        "cache_ffn_conv": nrm(ks[3], (DEPTH, DEC_BATCH, CONV_W - 1, D_FF)),
        "cache_k": nrm(ks[4], (DEC_BATCH, PAST_LEN, N_HEADS, 2 * HEAD_DIM)),
        "cache_v": nrm(ks[5], (DEC_BATCH, PAST_LEN, N_HEADS, V_HEAD)),
        "norm_mix": 1.0 + nrm(ks[6], (DEPTH, D_MODEL), 0.05),
        "pool_w": nrm(ks[7], (N_A, N_POOL_GROUPS, POOL_GROUP, POOL_GROUP), POOL_GROUP ** -0.5),
        "pool_scale": 0.5 + nrm(ks[8], (N_A, D_MODEL), 0.1),
        "norm_kv": 1.0 + nrm(ks[9], (D_MODEL,), 0.05),
        "w_k": nrm(ks[10], (D_MODEL, QK_WIDTH), D_MODEL ** -0.5),
        "w_v": nrm(ks[11], (D_MODEL, V_WIDTH), D_MODEL ** -0.5),
        "w_q": nrm(ks[12], (N_B, D_MODEL, QK_WIDTH), D_MODEL ** -0.5),
        "lam_q1": nrm(ks[13], (N_B, HEAD_DIM), 0.1),
        "lam_k1": nrm(ks[14], (N_B, HEAD_DIM), 0.1),
        "lam_q2": nrm(ks[15], (N_B, HEAD_DIM), 0.1),
        "lam_k2": nrm(ks[16], (N_B, HEAD_DIM), 0.1),
        "subln": 1.0 + nrm(ks[17], (N_B, V_HEAD), 0.05),
        "w_o": nrm(ks[18], (N_B, V_WIDTH, D_MODEL), V_WIDTH ** -0.5),
        "norm_ffn": 1.0 + nrm(ks[19], (DEPTH, D_MODEL), 0.05),
        "w_gate": nrm(ks[20], (DEPTH, D_MODEL, D_FF), D_MODEL ** -0.5),
        "w_up": nrm(ks[21], (DEPTH, D_MODEL, D_FF), D_MODEL ** -0.5),
        "conv_w": nrm(ks[22], (DEPTH, CONV_W, D_FF), CONV_W ** -0.5),
        "conv_b": nrm(ks[23], (DEPTH, D_FF), 0.01),
        "w_down": nrm(ks[24], (DEPTH, D_FF, D_MODEL), D_FF ** -0.5),
        "norm_final": 1.0 + nrm(ks[25], (D_MODEL,), 0.05),
    }


def reference(x_prompt, x_sample, cache_pool, cache_ffn_conv, cache_k, cache_v, norm_mix, pool_w,
              pool_scale, norm_kv, w_k, w_v, w_q, lam_q1, lam_k1, lam_q2, lam_k2, subln, w_o,
              norm_ffn, w_gate, w_up, conv_w, conv_b, w_down, norm_final):
    trunk = functools.partial(
        _trunk, norm_mix=norm_mix, pool_w=pool_w, pool_scale=pool_scale, norm_kv=norm_kv,
        w_k=w_k, w_v=w_v, w_q=w_q, lam_q1=lam_q1, lam_k1=lam_k1, lam_q2=lam_q2, lam_k2=lam_k2,
        subln=subln, w_o=w_o, norm_ffn=norm_ffn, w_gate=w_gate, w_up=w_up, conv_w=conv_w,
        conv_b=conv_b, w_down=w_down, norm_final=norm_final)

    bp, tp, _ = x_prompt.shape
    pos_p = jnp.arange(tp, dtype=jnp.int32)
    pool_hist_p = jnp.zeros((N_A, bp, POOL_HIST, D_MODEL), x_prompt.dtype)
    ffn_hist_p = jnp.zeros((DEPTH, bp, CONV_W - 1, D_FF), x_prompt.dtype)
    y_prompt, pool_state_prompt, ffn_state_prompt, k_prompt, v_prompt = trunk(
        x_prompt, pos_p, pool_hist_p, ffn_hist_p, None, None)

    ts = x_sample.shape[1]
    pos_s = PAST_LEN + jnp.arange(ts, dtype=jnp.int32)
    y_sample, pool_state_sample, ffn_state_sample, k_sample, v_sample = trunk(
        x_sample, pos_s, cache_pool, cache_ffn_conv, cache_k, cache_v)

    return (y_prompt, y_sample, pool_state_prompt, pool_state_sample, ffn_state_prompt,
            ffn_state_sample, k_prompt, v_prompt, k_sample, v_sample)
```

```python
import functools
import math

import jax
import jax.numpy as jnp
from jax import lax
from jax.experimental import pallas as pl
from jax.experimental.pallas import tpu as pltpu

F32 = jnp.float32
BF16 = jnp.bfloat16

CHUNK = 64
POOL_WINDOWS = (2, 4, 8, 16)
POOL_HIST = max(POOL_WINDOWS) - 1
POOL_HALO = POOL_HIST + 1
CONV_W = 3
CONV_HALO = 8
HEAD_DIM = 128
ROPE_THETA = 10000.0
EPS = 1e-5
NEG_BIG = -1e30
LANES = 128
VMEM_LIMIT = 56 * 1024 * 1024


def _params(*sem):
    return pltpu.CompilerParams(dimension_semantics=sem, vmem_limit_bytes=VMEM_LIMIT)


def _dot(a, b):
    return jnp.dot(a, b, preferred_element_type=F32)


def _dot_nt(a, b):
    return lax.dot_general(a, b, (((1,), (1,)), ((), ())), preferred_element_type=F32)


def _rms(x, gain):
    return x * lax.rsqrt(jnp.mean(x * x, axis=-1, keepdims=True) + EPS) * gain


def _lam_init(layer):
    return 0.8 - 0.6 * math.exp(-0.3 * layer)


def _rope_table_kernel(cos_ref, sin_ref, *, pos0):
    rows = cos_ref.shape[0]
    half = HEAD_DIM // 2
    row = lax.broadcasted_iota(jnp.int32, (rows, HEAD_DIM), 0) + pl.program_id(0) * rows
    lane = lax.broadcasted_iota(jnp.int32, (rows, HEAD_DIM), 1)
    idx = jnp.where(lane < half, lane, lane - half).astype(F32)
    inv = jnp.exp(-math.log(ROPE_THETA) * idx / half)
    ang = (row + pos0).astype(F32) * inv
    cos_ref[...] = jnp.cos(ang)
    sin = jnp.sin(ang)
    sin_ref[...] = jnp.where(lane < half, -sin, sin)


def _rope_tables(seq, pos0):
    rows = min(seq, 2048)
    assert seq % rows == 0
    return pl.pallas_call(
        functools.partial(_rope_table_kernel, pos0=pos0),
        grid=(seq // rows,),
        out_specs=[pl.BlockSpec((rows, HEAD_DIM), lambda i: (i, 0))] * 2,
        out_shape=[jax.ShapeDtypeStruct((seq, HEAD_DIM), F32)] * 2,
        compiler_params=_params("parallel"),
    )()


def _pool_kernel(x_ref, hist_ref, gain_ref, w_ref, scale_ref, out_ref, state_ref, ext_s, *, pos0):
    nb, tm, d = x_ref.shape
    group = d // len(POOL_WINDOWS)
    t = pl.program_id(1)

    @pl.when(t == 0)
    def _():
        ext_s[:, :POOL_HALO, :] = hist_ref[...]

    x = x_ref[...]
    ext_s[:, POOL_HALO:, :] = _rms(x, gain_ref[...])
    pos = pos0 + t * tm + lax.broadcasted_iota(jnp.int32, (1, tm, 1), 1)
    for g, w in enumerate(POOL_WINDOWS):
        sl = slice(g * group, (g + 1) * group)
        ext = ext_s[:, :, sl]
        s = ext
        k = 1
        while k < w:
            s = s + pltpu.roll(s, k, axis=1)
            k *= 2
        inv_cnt = 1.0 / jnp.minimum(pos + 1, w).astype(F32)
        dd = s[:, POOL_HALO:, :] * inv_cnt - ext[:, POOL_HALO:, :]
        mix = _dot(dd.reshape(nb * tm, group).astype(BF16), w_ref[g]).reshape(nb, tm, group)
        out_ref[:, :, sl] = x[:, :, sl] + mix * scale_ref[:, sl]
    state_ref[...] = ext_s[:, tm + 1:, :]
    ext_s[:, :POOL_HALO, :] = ext_s[:, tm:, :]


def _pool_layer(x, hist, gain, w, scale, *, pos0, nb, tm):
    b, seq, d = x.shape
    ng = len(POOL_WINDOWS)
    group = d // ng
    hist = jnp.pad(hist, ((0, 0), (POOL_HALO - POOL_HIST, 0), (0, 0)))
    return pl.pallas_call(
        functools.partial(_pool_kernel, pos0=pos0),
        grid=(b // nb, seq // tm),
        in_specs=[
            pl.BlockSpec((nb, tm, d), lambda i, t: (i, t, 0)),
            pl.BlockSpec((nb, POOL_HALO, d), lambda i, t: (i, 0, 0)),
            pl.BlockSpec((1, d), lambda i, t: (0, 0)),
            pl.BlockSpec((ng, group, group), lambda i, t: (0, 0, 0)),
            pl.BlockSpec((1, d), lambda i, t: (0, 0)),
        ],
        out_specs=[
            pl.BlockSpec((nb, tm, d), lambda i, t: (i, t, 0)),
            pl.BlockSpec((nb, POOL_HIST, d), lambda i, t: (i, 0, 0)),
        ],
        out_shape=[
            jax.ShapeDtypeStruct((b, seq, d), F32),
            jax.ShapeDtypeStruct((b, POOL_HIST, d), F32),
        ],
        scratch_shapes=[pltpu.VMEM((nb, POOL_HALO + tm, d), F32)],
        compiler_params=_params("parallel", "arbitrary"),
    )(x, hist, gain.reshape(1, d), w, scale.reshape(1, d))


def _ffn_kernel(x_ref, hist_ref, gain_ref, wg_ref, wu_ref, cw_ref, cb_ref, wd_ref, gfin_ref,
                out_ref, state_ref, h_s, tail_s, *, final):
    nb, tm, d = x_ref.shape
    tf = wg_ref.shape[1]
    t = pl.program_id(1)
    j = pl.program_id(2)

    @pl.when(j == 0)
    def _():
        h_s[...] = _rms(x_ref[...], gain_ref[...]).reshape(nb * tm, d).astype(BF16)

    @pl.when(t == 0)
    def _():
        tail_s[j] = hist_ref[...]

    h = h_s[...]
    gate = _dot(h, wg_ref[...]).reshape(nb, tm, tf)
    up = _dot(h, wu_ref[...]).reshape(nb, tm, tf)
    ext = jnp.concatenate([tail_s[j], gate], axis=1)
    prev1 = pltpu.roll(ext, 1, axis=1)[:, CONV_HALO:, :]
    prev2 = pltpu.roll(ext, 2, axis=1)[:, CONV_HALO:, :]
    cw = cw_ref[...]
    c = cb_ref[...] + prev2 * cw[0:1] + prev1 * cw[1:2] + gate * cw[2:3]
    act = c * (1.0 / (1.0 + jnp.exp(-c))) * up
    tail_s[j] = gate[:, tm - CONV_HALO:, :]
    state_ref[...] = gate[:, None, tm - (CONV_W - 1):, :]
    down = _dot(act.reshape(nb * tm, tf).astype(BF16), wd_ref[...]).reshape(nb, tm, d)

    @pl.when(j == 0)
    def _():
        out_ref[...] = x_ref[...] + down

    @pl.when(j > 0)
    def _():
        out_ref[...] += down

    if final:
        @pl.when(j == pl.num_programs(2) - 1)
        def _():
            out_ref[...] = _rms(out_ref[...], gfin_ref[...])


def _ffn_layer(x, hist, gain, wg, wu, cw, cb, wd, gfin, *, final, nb, tm, tf):
    b, seq, d = x.shape
    dff = wg.shape[1]
    nf = dff // tf
    hist = jnp.pad(hist, ((0, 0), (CONV_HALO - (CONV_W - 1), 0), (0, 0)))
    out, tails = pl.pallas_call(
        functools.partial(_ffn_kernel, final=final),
        grid=(b // nb, seq // tm, nf),
        in_specs=[
            pl.BlockSpec((nb, tm, d), lambda i, t, j: (i, t, 0)),
            pl.BlockSpec((nb, CONV_HALO, tf), lambda i, t, j: (i, 0, j)),
            pl.BlockSpec((1, d), lambda i, t, j: (0, 0)),
            pl.BlockSpec((d, tf), lambda i, t, j: (0, j)),
            pl.BlockSpec((d, tf), lambda i, t, j: (0, j)),
            pl.BlockSpec((CONV_W, tf), lambda i, t, j: (0, j)),
            pl.BlockSpec((1, tf), lambda i, t, j: (0, j)),
            pl.BlockSpec((tf, d), lambda i, t, j: (j, 0)),
            pl.BlockSpec((1, d), lambda i, t, j: (0, 0)),
        ],
        out_specs=[
            pl.BlockSpec((nb, tm, d), lambda i, t, j: (i, t, 0)),
            pl.BlockSpec((nb, 1, CONV_W - 1, tf), lambda i, t, j: (i, t, 0, j)),
        ],
        out_shape=[
            jax.ShapeDtypeStruct((b, seq, d), F32),
            jax.ShapeDtypeStruct((b, seq // tm, CONV_W - 1, dff), F32),
        ],
        scratch_shapes=[
            pltpu.VMEM((nb * tm, d), BF16),
            pltpu.VMEM((nf, nb, CONV_HALO, tf), F32),
        ],
        compiler_params=_params("parallel", "arbitrary", "arbitrary"),
    )(x, hist, gain.reshape(1, d), wg, wu, cw, cb.reshape(1, dff), wd, gfin.reshape(1, d))
    return out, tails[:, -1]


def _qkv_kernel(x_ref, gkv_ref, gq_ref, wk_ref, wv_ref, wq_ref, cos_ref, sin_ref,
                k_ref, v_ref, kb_ref, vb_ref, qb_ref, hk_s, hq_s):
    tn = wk_ref.shape[1]

    @pl.when(pl.program_id(1) == 0)
    def _():
        x = x_ref[...]
        xn = x * lax.rsqrt(jnp.mean(x * x, axis=-1, keepdims=True) + EPS)
        hk_s[...] = (xn * gkv_ref[...]).astype(BF16)
        hq_s[...] = (xn * gq_ref[...]).astype(BF16)

    cos = cos_ref[...]
    sin = sin_ref[...]
    hk = hk_s[...]
    k = _dot(hk, wk_ref[...])
    q = _dot(hq_s[...], wq_ref[...])
    v = _dot(hk, wv_ref[...])
    v_ref[...] = v
    vb_ref[...] = v.astype(BF16)
    for c in range(tn // HEAD_DIM):
        sl = slice(c * HEAD_DIM, (c + 1) * HEAD_DIM)
        kc = k[:, sl]
        kc = kc * cos + pltpu.roll(kc, HEAD_DIM // 2, axis=1) * sin
        k_ref[:, sl] = kc
        kb_ref[:, sl] = kc.astype(BF16)
        qc = q[:, sl]
        qb_ref[:, sl] = (qc * cos + pltpu.roll(qc, HEAD_DIM // 2, axis=1) * sin).astype(BF16)


def _qkv_proj(x, gkv, gq, wk, wv, wq, cos, sin, *, tm, tn):
    n, d = x.shape
    width = wk.shape[1]
    ncos = cos.shape[0] // tm
    row = lambda i, j: (i, 0)
    col = lambda i, j: (0, j)
    tile = lambda i, j: (i, j)
    return pl.pallas_call(
        _qkv_kernel,
        grid=(n // tm, width // tn),
        in_specs=[
            pl.BlockSpec((tm, d), row),
            pl.BlockSpec((1, d), lambda i, j: (0, 0)),
            pl.BlockSpec((1, d), lambda i, j: (0, 0)),
            pl.BlockSpec((d, tn), col),
            pl.BlockSpec((d, tn), col),
            pl.BlockSpec((d, tn), col),
            pl.BlockSpec((tm, HEAD_DIM), lambda i, j: (i % ncos, 0)),
            pl.BlockSpec((tm, HEAD_DIM), lambda i, j: (i % ncos, 0)),
        ],
        out_specs=[pl.BlockSpec((tm, tn), tile)] * 5,
        out_shape=[
            jax.ShapeDtypeStruct((n, width), F32),
            jax.ShapeDtypeStruct((n, width), F32),
            jax.ShapeDtypeStruct((n, width), BF16),
            jax.ShapeDtypeStruct((n, width), BF16),
            jax.ShapeDtypeStruct((n, width), BF16),
        ],
        scratch_shapes=[pltpu.VMEM((tm, d), BF16), pltpu.VMEM((tm, d), BF16)],
        compiler_params=_params("parallel", "arbitrary"),
    )(x, gkv.reshape(1, d), gq.reshape(1, d), wk, wv, wq, cos, sin)


def _lam(lq1_ref, lk1_ref, lq2_ref, lk2_ref, lam_init):
    a = jnp.exp(jnp.sum(lq1_ref[...] * lk1_ref[...], axis=-1, keepdims=True))
    b = jnp.exp(jnp.sum(lq2_ref[...] * lk2_ref[...], axis=-1, keepdims=True))
    return a - b + lam_init


def _sub_norm(o, subln_ref, lam_init):
    return _rms(o, subln_ref[...]) * (1.0 - lam_init)


def _attn_prompt_kernel(q_ref, k_ref, v_ref, lq1_ref, lk1_ref, lq2_ref, lk2_ref, subln_ref,
                        o_ref, m_s, l_s, acc_s, *, lam_init):
    tq = q_ref.shape[0]
    tk = tq
    vw = v_ref.shape[1]
    i = pl.program_id(2)
    scale = HEAD_DIM ** -0.5

    m_s[...] = jnp.full(m_s.shape, NEG_BIG, F32)
    l_s[...] = jnp.zeros(l_s.shape, F32)
    acc_s[...] = jnp.zeros(acc_s.shape, F32)
    q = q_ref[...]

    def step(j, masked):
        off = pl.multiple_of(j * tk, tk)
        k = k_ref[pl.ds(off, tk), :]
        v = v_ref[pl.ds(off, tk), :]
        for c in range(2):
            sl = slice(c * HEAD_DIM, (c + 1) * HEAD_DIM)
            s = _dot_nt(q[:, sl], k[:, sl]) * scale
            if masked:
                qc = lax.broadcasted_iota(jnp.int32, (tq, tk), 0) // CHUNK
                kc = lax.broadcasted_iota(jnp.int32, (tq, tk), 1) // CHUNK
                s = jnp.where(kc <= qc, s, NEG_BIG)
            m_prev = m_s[c]
            m_next = jnp.maximum(m_prev, jnp.max(s, axis=1, keepdims=True))
            p = jnp.exp(s - pltpu.repeat(m_next, tk // LANES, axis=1))
            alpha = jnp.exp(m_prev - m_next)
            l_s[c] = alpha * l_s[c] + jnp.sum(p, axis=1, keepdims=True)
            m_s[c] = m_next
            acc_s[c] = acc_s[c] * pltpu.repeat(alpha, vw // LANES, axis=1) + _dot(p.astype(BF16), v)

    def body(j, carry):
        step(j, False)
        return carry

    lax.fori_loop(0, i, body, 0)
    step(i, True)

    o0 = acc_s[0] / pltpu.repeat(l_s[0], vw // LANES, axis=1)
    o1 = acc_s[1] / pltpu.repeat(l_s[1], vw // LANES, axis=1)
    o = o0 - _lam(lq1_ref, lk1_ref, lq2_ref, lk2_ref, lam_init) * o1
    o_ref[...] = _sub_norm(o, subln_ref, lam_init).astype(o_ref.dtype)


def _attn_prompt(q, k, v, lq1, lk1, lq2, lk2, subln, *, batch, seq, lam_init, tq):
    n, width = q.shape
    vw = subln.shape[-1]
    heads = width // vw
    nq = seq // tq
    vec = lambda a: a.reshape(1, -1)
    small = lambda w: pl.BlockSpec((1, w), lambda b, h, i: (0, 0))
    return pl.pallas_call(
        functools.partial(_attn_prompt_kernel, lam_init=lam_init),
        grid=(batch, heads, nq),
        in_specs=[
            pl.BlockSpec((tq, vw), lambda b, h, i: (b * nq + i, h)),
            pl.BlockSpec((seq, vw), lambda b, h, i: (b, h)),
            pl.BlockSpec((seq, vw), lambda b, h, i: (b, h)),
            small(HEAD_DIM), small(HEAD_DIM), small(HEAD_DIM), small(HEAD_DIM), small(vw),
        ],
        out_specs=pl.BlockSpec((tq, vw), lambda b, h, i: (b * nq + i, h)),
        out_shape=jax.ShapeDtypeStruct((n, width), BF16),
        scratch_shapes=[
            pltpu.VMEM((2, tq, LANES), F32),
            pltpu.VMEM((2, tq, LANES), F32),
            pltpu.VMEM((2, tq, vw), F32),
        ],
        compiler_params=_params("parallel", "parallel", "arbitrary"),
    )(q, k, v, vec(lq1), vec(lk1), vec(lq2), vec(lk2), vec(subln))


def _attn_sample_kernel(q_ref, kn_ref, vn_ref, ck_ref, cv_ref, lq1_ref, lk1_ref, lq2_ref, lk2_ref,
                        subln_ref, o_ref, *, lam_init):
    scale = HEAD_DIM ** -0.5
    q = q_ref[...]
    kc = ck_ref[0].astype(BF16)
    kn = kn_ref[...]
    lam = _lam(lq1_ref, lk1_ref, lq2_ref, lk2_ref, lam_init)
    wc = []
    wn = []
    for c in range(2):
        sl = slice(c * HEAD_DIM, (c + 1) * HEAD_DIM)
        sc = _dot_nt(q[:, sl], kc[:, sl]) * scale
        sn = _dot_nt(q[:, sl], kn[:, sl]) * scale
        m = jnp.maximum(jnp.max(sc, axis=1, keepdims=True), jnp.max(sn, axis=1, keepdims=True))
        pc = jnp.exp(sc - m)
        pn = jnp.exp(sn - m)
        den = jnp.sum(pc, axis=1, keepdims=True) + jnp.sum(pn, axis=1, keepdims=True)
        wc.append(pc / den)
        wn.append(pn / den)
    o = (_dot((wc[0] - lam * wc[1]).astype(BF16), cv_ref[0].astype(BF16))
         + _dot((wn[0] - lam * wn[1]).astype(BF16), vn_ref[...]))
    o_ref[...] = _sub_norm(o, subln_ref, lam_init).astype(o_ref.dtype)


def _attn_sample(q, kn, vn, cache_k, cache_v, lq1, lk1, lq2, lk2, subln, *, ts, lam_init):
    n, width = q.shape
    b, past, _ = cache_k.shape
    vw = subln.shape[-1]
    heads = width // vw
    vec = lambda a: a.reshape(1, -1)
    small = lambda w: pl.BlockSpec((1, w), lambda i, h: (0, 0))
    new = pl.BlockSpec((ts, vw), lambda i, h: (i, h))
    old = pl.BlockSpec((1, past, vw), lambda i, h: (i, 0, h))
    return pl.pallas_call(
        functools.partial(_attn_sample_kernel, lam_init=lam_init),
        grid=(b, heads),
        in_specs=[new, new, new, old, old,
                  small(HEAD_DIM), small(HEAD_DIM), small(HEAD_DIM), small(HEAD_DIM), small(vw)],
        out_specs=new,
        out_shape=jax.ShapeDtypeStruct((n, width), BF16),
        compiler_params=_params("parallel", "parallel"),
    )(q, kn, vn, cache_k, cache_v, vec(lq1), vec(lk1), vec(lq2), vec(lk2), vec(subln))


def _oproj_kernel(o_ref, w_ref, x_ref, out_ref):
    out_ref[...] = x_ref[...] + _dot(o_ref[...], w_ref[...])


def _oproj(o, w, x, *, tm, tn):
    n, d = x.shape
    width = o.shape[1]
    return pl.pallas_call(
        _oproj_kernel,
        grid=(n // tm, d // tn),
        in_specs=[
            pl.BlockSpec((tm, width), lambda i, j: (i, 0)),
            pl.BlockSpec((width, tn), lambda i, j: (0, j)),
            pl.BlockSpec((tm, tn), lambda i, j: (i, j)),
        ],
        out_specs=pl.BlockSpec((tm, tn), lambda i, j: (i, j)),
        out_shape=jax.ShapeDtypeStruct((n, d), F32),
        compiler_params=_params("parallel", "arbitrary"),
    )(o, w, x)


def _tiles(batch, seq):
    if seq >= 512:
        return dict(nb=1, tm=512)
    return dict(nb=batch, tm=seq)


def _trunk(x, pos0, pool_hist, ffn_hist, past_k, past_v, wts):
    b, seq, d = x.shape
    tl = _tiles(b, seq)
    rows = tl["nb"] * tl["tm"]
    heads_w = wts["w_k"].shape[1]
    vw = wts["subln"].shape[-1]
    heads = heads_w // vw
    dff = wts["w_gate"].shape[-1]
    tf = 512 if dff % 512 == 0 else LANES
    assert len(pool_hist) == 1 and wts["w_q"].shape[0] == 1 and vw == 2 * HEAD_DIM

    x1, pool_state = _pool_layer(x, pool_hist[0], wts["norm_mix"][0], wts["pool_w"][0],
                                 wts["pool_scale"][0], pos0=pos0, **tl)
    x2, ffn_state0 = _ffn_layer(x1, ffn_hist[0], wts["norm_ffn"][0], wts["w_gate"][0], wts["w_up"][0],
                                wts["conv_w"][0], wts["conv_b"][0], wts["w_down"][0],
                                wts["norm_final"], final=False, tf=tf, **tl)

    cos, sin = _rope_tables(seq, pos0)
    if rows > seq:
        cos = jnp.tile(cos, (rows // seq, 1))
        sin = jnp.tile(sin, (rows // seq, 1))
    x2f = x2.reshape(b * seq, d)
    k, v, kb, vb, qb = _qkv_proj(x2f, wts["norm_kv"], wts["norm_mix"][1], wts["w_k"], wts["w_v"],
                                 wts["w_q"][0], cos, sin, tm=rows, tn=512)
    lam_init = _lam_init(1)
    lam_args = (wts["lam_q1"][0], wts["lam_k1"][0], wts["lam_q2"][0], wts["lam_k2"][0], wts["subln"][0])
    if past_k is None:
        assert seq % 512 == 0
        o = _attn_prompt(qb, kb, vb, *lam_args, batch=b, seq=seq, lam_init=lam_init, tq=512)
    else:
        past = past_k.shape[1]
        assert past % CHUNK == 0 and seq <= CHUNK
        o = _attn_sample(qb, kb, vb, past_k.reshape(b, past, heads_w), past_v.reshape(b, past, heads_w),
                         *lam_args, ts=seq, lam_init=lam_init)
    x3 = _oproj(o, wts["w_o"][0], x2f, tm=rows, tn=512).reshape(b, seq, d)
    y, ffn_state1 = _ffn_layer(x3, ffn_hist[1], wts["norm_ffn"][1], wts["w_gate"][1], wts["w_up"][1],
                               wts["conv_w"][1], wts["conv_b"][1], wts["w_down"][1],
                               wts["norm_final"], final=True, tf=tf, **tl)
    return (y, pool_state[None], jnp.stack([ffn_state0, ffn_state1]),
            k.reshape(b, seq, heads, vw), v.reshape(b, seq, heads, vw))


def kernel(x_prompt, x_sample, cache_pool, cache_ffn_conv, cache_k, cache_v, norm_mix, pool_w, pool_scale, norm_kv, w_k, w_v, w_q, lam_q1, lam_k1, lam_q2, lam_k2, subln, w_o, norm_ffn, w_gate, w_up, conv_w, conv_b, w_down, norm_final):
    wts = dict(norm_mix=norm_mix, pool_w=pool_w.astype(BF16), pool_scale=pool_scale, norm_kv=norm_kv,
               w_k=w_k.astype(BF16), w_v=w_v.astype(BF16), w_q=w_q.astype(BF16),
               lam_q1=lam_q1, lam_k1=lam_k1, lam_q2=lam_q2, lam_k2=lam_k2, subln=subln,
               w_o=w_o.astype(BF16), norm_ffn=norm_ffn, w_gate=w_gate.astype(BF16),
               w_up=w_up.astype(BF16), conv_w=conv_w, conv_b=conv_b, w_down=w_down.astype(BF16),
               norm_final=norm_final)
    bp, _, d = x_prompt.shape
    dff = w_gate.shape[-1]
    n_a = cache_pool.shape[0]
    depth = cache_ffn_conv.shape[0]
    pool_hist_p = jnp.zeros((n_a, bp, POOL_HIST, d), F32)
    ffn_hist_p = jnp.zeros((depth, bp, CONV_W - 1, dff), F32)
    y_p, pool_p, ffn_p, k_p, v_p = _trunk(x_prompt, 0, pool_hist_p, ffn_hist_p, None, None, wts)
    y_s, pool_s, ffn_s, k_s, v_s = _trunk(x_sample, cache_k.shape[1], cache_pool, cache_ffn_conv,
                                          cache_k, cache_v, wts)
    return (y_p, y_s, pool_p, pool_s, ffn_p, ffn_s, k_p, v_p, k_s, v_s)
```

```python
import functools
import math

import jax
import jax.numpy as jnp
from jax import lax
from jax.experimental import pallas as pl
from jax.experimental.pallas import tpu as pltpu

F32 = jnp.float32
BF16 = jnp.bfloat16

CHUNK = 64
POOL_WINDOWS = (2, 4, 8, 16)
POOL_HIST = max(POOL_WINDOWS) - 1
POOL_HALO = POOL_HIST + 1
CONV_W = 3
CONV_HALO = 8
HEAD_DIM = 128
ROPE_THETA = 10000.0
EPS = 1e-5
NEG_BIG = -1e30
QUERY_SCALE = HEAD_DIM ** -0.5 * math.log2(math.e)
LANES = 128
KEY_BLOCK = 256
VMEM_LIMIT = 56 * 1024 * 1024


def _params(*sem):
    return pltpu.CompilerParams(dimension_semantics=sem, vmem_limit_bytes=VMEM_LIMIT)


def _dot(a, b):
    return jnp.dot(a, b, preferred_element_type=F32)


def _dot_nt(a, b):
    return lax.dot_general(a, b, (((1,), (1,)), ((), ())), preferred_element_type=F32)


def _rms(x, gain):
    return x * lax.rsqrt(jnp.mean(x * x, axis=-1, keepdims=True) + EPS) * gain


def _lam_init(layer):
    return 0.8 - 0.6 * math.exp(-0.3 * layer)


def _rope_table_kernel(cos_ref, sin_ref, *, pos0):
    rows = cos_ref.shape[0]
    half = HEAD_DIM // 2
    row = lax.broadcasted_iota(jnp.int32, (rows, HEAD_DIM), 0) + pl.program_id(0) * rows
    lane = lax.broadcasted_iota(jnp.int32, (rows, HEAD_DIM), 1)
    idx = jnp.where(lane < half, lane, lane - half).astype(F32)
    inv = jnp.exp(-math.log(ROPE_THETA) * idx / half)
    ang = (row + pos0).astype(F32) * inv
    cos_ref[...] = jnp.cos(ang)
    sin = jnp.sin(ang)
    sin_ref[...] = jnp.where(lane < half, -sin, sin)


def _rope_tables(seq, pos0):
    rows = min(seq, 2048)
    assert seq % rows == 0
    return pl.pallas_call(
        functools.partial(_rope_table_kernel, pos0=pos0),
        grid=(seq // rows,),
        out_specs=[pl.BlockSpec((rows, HEAD_DIM), lambda i: (i, 0))] * 2,
        out_shape=[jax.ShapeDtypeStruct((seq, HEAD_DIM), F32)] * 2,
        compiler_params=_params("parallel"),
    )()


def _pool_kernel(x_ref, hist_ref, gain_ref, w_ref, scale_ref, out_ref, state_ref, ext_s, *, pos0):
    nb, tm, d = x_ref.shape
    group = d // len(POOL_WINDOWS)
    t = pl.program_id(1)

    @pl.when(t == 0)
    def _():
        ext_s[:, :POOL_HALO, :] = hist_ref[...]

    x = x_ref[...]
    ext_s[:, POOL_HALO:, :] = _rms(x, gain_ref[...])
    pos = pos0 + t * tm + lax.broadcasted_iota(jnp.int32, (1, tm, 1), 1)
    for g, w in enumerate(POOL_WINDOWS):
        sl = slice(g * group, (g + 1) * group)
        ext = ext_s[:, :, sl]
        s = ext
        k = 1
        while k < w:
            s = s + pltpu.roll(s, k, axis=1)
            k *= 2
        inv_cnt = 1.0 / jnp.minimum(pos + 1, w).astype(F32)
        dd = s[:, POOL_HALO:, :] * inv_cnt - ext[:, POOL_HALO:, :]
        mix = _dot(dd.reshape(nb * tm, group).astype(BF16), w_ref[g]).reshape(nb, tm, group)
        out_ref[:, :, sl] = x[:, :, sl] + mix * scale_ref[:, sl]
    state_ref[...] = ext_s[:, tm + 1:, :]
    ext_s[:, :POOL_HALO, :] = ext_s[:, tm:, :]


def _pool_layer(x, hist, gain, w, scale, *, pos0, nb, tm):
    b, seq, d = x.shape
    ng = len(POOL_WINDOWS)
    group = d // ng
    hist = jnp.pad(hist, ((0, 0), (POOL_HALO - POOL_HIST, 0), (0, 0)))
    return pl.pallas_call(
        functools.partial(_pool_kernel, pos0=pos0),
        grid=(b // nb, seq // tm),
        in_specs=[
            pl.BlockSpec((nb, tm, d), lambda i, t: (i, t, 0)),
            pl.BlockSpec((nb, POOL_HALO, d), lambda i, t: (i, 0, 0)),
            pl.BlockSpec((1, d), lambda i, t: (0, 0)),
            pl.BlockSpec((ng, group, group), lambda i, t: (0, 0, 0)),
            pl.BlockSpec((1, d), lambda i, t: (0, 0)),
        ],
        out_specs=[
            pl.BlockSpec((nb, tm, d), lambda i, t: (i, t, 0)),
            pl.BlockSpec((nb, POOL_HIST, d), lambda i, t: (i, 0, 0)),
        ],
        out_shape=[
            jax.ShapeDtypeStruct((b, seq, d), F32),
            jax.ShapeDtypeStruct((b, POOL_HIST, d), F32),
        ],
        scratch_shapes=[pltpu.VMEM((nb, POOL_HALO + tm, d), F32)],
        compiler_params=_params("parallel", "arbitrary"),
    )(x, hist, gain.reshape(1, d), w, scale.reshape(1, d))


def _ffn_kernel(x_ref, hist_ref, gain_ref, wg_ref, wu_ref, cw_ref, cb_ref, wd_ref, gfin_ref,
                out_ref, state_ref, h_s, tail_s, *, final):
    nb, tm, d = x_ref.shape
    tf = wg_ref.shape[1]
    t = pl.program_id(1)
    j = pl.program_id(2)

    @pl.when(j == 0)
    def _():
        h_s[...] = _rms(x_ref[...], gain_ref[...]).reshape(nb * tm, d).astype(BF16)

    @pl.when(t == 0)
    def _():
        tail_s[j] = hist_ref[...]

    h = h_s[...]
    gate = _dot(h, wg_ref[...]).reshape(nb, tm, tf)
    up = _dot(h, wu_ref[...]).reshape(nb, tm, tf)
    ext = jnp.concatenate([tail_s[j], gate], axis=1)
    prev1 = pltpu.roll(ext, 1, axis=1)[:, CONV_HALO:, :]
    prev2 = pltpu.roll(ext, 2, axis=1)[:, CONV_HALO:, :]
    cw = cw_ref[...]
    c = cb_ref[...] + prev2 * cw[0:1] + prev1 * cw[1:2] + gate * cw[2:3]
    act = c * (1.0 / (1.0 + jnp.exp(-c))) * up
    tail_s[j] = gate[:, tm - CONV_HALO:, :]
    state_ref[...] = gate[:, None, tm - (CONV_W - 1):, :]
    down = _dot(act.reshape(nb * tm, tf).astype(BF16), wd_ref[...]).reshape(nb, tm, d)

    @pl.when(j == 0)
    def _():
        out_ref[...] = x_ref[...] + down

    @pl.when(j > 0)
    def _():
        out_ref[...] += down

    if final:
        @pl.when(j == pl.num_programs(2) - 1)
        def _():
            out_ref[...] = _rms(out_ref[...], gfin_ref[...])


def _ffn_layer(x, hist, gain, wg, wu, cw, cb, wd, gfin, *, final, nb, tm, tf):
    b, seq, d = x.shape
    dff = wg.shape[1]
    nf = dff // tf
    hist = jnp.pad(hist, ((0, 0), (CONV_HALO - (CONV_W - 1), 0), (0, 0)))
    out, tails = pl.pallas_call(
        functools.partial(_ffn_kernel, final=final),
        grid=(b // nb, seq // tm, nf),
        in_specs=[
            pl.BlockSpec((nb, tm, d), lambda i, t, j: (i, t, 0)),
            pl.BlockSpec((nb, CONV_HALO, tf), lambda i, t, j: (i, 0, j)),
            pl.BlockSpec((1, d), lambda i, t, j: (0, 0)),
            pl.BlockSpec((d, tf), lambda i, t, j: (0, j)),
            pl.BlockSpec((d, tf), lambda i, t, j: (0, j)),
            pl.BlockSpec((CONV_W, tf), lambda i, t, j: (0, j)),
            pl.BlockSpec((1, tf), lambda i, t, j: (0, j)),
            pl.BlockSpec((tf, d), lambda i, t, j: (j, 0)),
            pl.BlockSpec((1, d), lambda i, t, j: (0, 0)),
        ],
        out_specs=[
            pl.BlockSpec((nb, tm, d), lambda i, t, j: (i, t, 0)),
            pl.BlockSpec((nb, 1, CONV_W - 1, tf), lambda i, t, j: (i, t, 0, j)),
        ],
        out_shape=[
            jax.ShapeDtypeStruct((b, seq, d), F32),
            jax.ShapeDtypeStruct((b, seq // tm, CONV_W - 1, dff), F32),
        ],
        scratch_shapes=[
            pltpu.VMEM((nb * tm, d), BF16),
            pltpu.VMEM((nf, nb, CONV_HALO, tf), F32),
        ],
        compiler_params=_params("parallel", "arbitrary", "arbitrary"),
    )(x, hist, gain.reshape(1, d), wg, wu, cw, cb.reshape(1, dff), wd, gfin.reshape(1, d))
    return out, tails[:, -1]


def _qkv_kernel(x_ref, gkv_ref, gq_ref, wk_ref, wv_ref, wq_ref, cos_ref, sin_ref,
                k_ref, v_ref, kb_ref, vb_ref, qb_ref, hk_s, hq_s):
    tn = wk_ref.shape[1]

    @pl.when(pl.program_id(1) == 0)
    def _():
        x = x_ref[...]
        xn = x * lax.rsqrt(jnp.mean(x * x, axis=-1, keepdims=True) + EPS)
        hk_s[...] = (xn * gkv_ref[...]).astype(BF16)
        hq_s[...] = (xn * gq_ref[...]).astype(BF16)

    cos = cos_ref[...]
    sin = sin_ref[...]
    hk = hk_s[...]
    k = _dot(hk, wk_ref[...])
    q = _dot(hq_s[...], wq_ref[...])
    v = _dot(hk, wv_ref[...])
    v_ref[...] = v
    vb_ref[...] = v.astype(BF16)
    for c in range(tn // HEAD_DIM):
        sl = slice(c * HEAD_DIM, (c + 1) * HEAD_DIM)
        kc = k[:, sl]
        kc = kc * cos + pltpu.roll(kc, HEAD_DIM // 2, axis=1) * sin
        k_ref[:, sl] = kc
        kb_ref[:, sl] = kc.astype(BF16)
        qc = q[:, sl]
        qc = qc * cos + pltpu.roll(qc, HEAD_DIM // 2, axis=1) * sin
        qb_ref[:, sl] = (qc * QUERY_SCALE).astype(BF16)


def _qkv_proj(x, gkv, gq, wk, wv, wq, cos, sin, *, tm, tn):
    n, d = x.shape
    width = wk.shape[1]
    ncos = cos.shape[0] // tm
    row = lambda i, j: (i, 0)
    col = lambda i, j: (0, j)
    tile = lambda i, j: (i, j)
    return pl.pallas_call(
        _qkv_kernel,
        grid=(n // tm, width // tn),
        in_specs=[
            pl.BlockSpec((tm, d), row),
            pl.BlockSpec((1, d), lambda i, j: (0, 0)),
            pl.BlockSpec((1, d), lambda i, j: (0, 0)),
            pl.BlockSpec((d, tn), col),
            pl.BlockSpec((d, tn), col),
            pl.BlockSpec((d, tn), col),
            pl.BlockSpec((tm, HEAD_DIM), lambda i, j: (i % ncos, 0)),
            pl.BlockSpec((tm, HEAD_DIM), lambda i, j: (i % ncos, 0)),
        ],
        out_specs=[pl.BlockSpec((tm, tn), tile)] * 5,
        out_shape=[
            jax.ShapeDtypeStruct((n, width), F32),
            jax.ShapeDtypeStruct((n, width), F32),
            jax.ShapeDtypeStruct((n, width), BF16),
            jax.ShapeDtypeStruct((n, width), BF16),
            jax.ShapeDtypeStruct((n, width), BF16),
        ],
        scratch_shapes=[pltpu.VMEM((tm, d), BF16), pltpu.VMEM((tm, d), BF16)],
        compiler_params=_params("parallel", "arbitrary"),
    )(x, gkv.reshape(1, d), gq.reshape(1, d), wk, wv, wq, cos, sin)


def _lam(lq1_ref, lk1_ref, lq2_ref, lk2_ref, lam_init):
    a = jnp.exp(jnp.sum(lq1_ref[...] * lk1_ref[...], axis=-1, keepdims=True))
    b = jnp.exp(jnp.sum(lq2_ref[...] * lk2_ref[...], axis=-1, keepdims=True))
    return a - b + lam_init


def _sub_norm(o, subln_ref, lam_init):
    return _rms(o, subln_ref[...]) * (1.0 - lam_init)


def _attn_prompt_kernel(q_ref, k_ref, v_ref, lq1_ref, lk1_ref, lq2_ref, lk2_ref, subln_ref,
                        o_ref, m_s, l_s, acc_s, sa_s, sb_s, ma_s, mb_s, *, lam_init):
    tq = q_ref.shape[0]
    tk = sa_s.shape[2]
    vw = v_ref.shape[1]
    i = pl.program_id(2)
    nf = i // (tk // tq)
    diag_shift = (i % (tk // tq)) * (tq // CHUNK)

    m_s[...] = jnp.full(m_s.shape, NEG_BIG, F32)
    l_s[...] = jnp.zeros(l_s.shape, F32)
    acc_s[...] = jnp.zeros(acc_s.shape, F32)

    def tile_at(u):
        return jnp.where(u == 0, nf, u - 1)

    def lane_fold(x, op):
        out = x[:, :LANES]
        for g in range(1, x.shape[1] // LANES):
            out = op(out, x[:, g * LANES:(g + 1) * LANES])
        return out

    def scores(t, buf, masked):
        s_buf, max_buf = buf
        for c in range(2):
            sl = slice(c * HEAD_DIM, (c + 1) * HEAD_DIM)
            q = q_ref[:, sl]
            fold = None
            for kb in range(tk // KEY_BLOCK):
                off = pl.multiple_of(t * tk + kb * KEY_BLOCK, KEY_BLOCK)
                s = _dot_nt(q, k_ref[pl.ds(off, KEY_BLOCK), sl])
                if masked:
                    shape = (tq, KEY_BLOCK)
                    qc = lax.broadcasted_iota(jnp.int32, shape, 0) // CHUNK + diag_shift
                    kc = lax.broadcasted_iota(jnp.int32, shape, 1) // CHUNK + kb * (KEY_BLOCK // CHUNK)
                    s = jnp.where(kc <= qc, s, NEG_BIG)
                s_buf[c, :, kb * KEY_BLOCK:(kb + 1) * KEY_BLOCK] = s
                part = lane_fold(s, jnp.maximum)
                fold = part if fold is None else jnp.maximum(fold, part)
            max_buf[c] = fold

    def absorb(t, buf):
        s_buf, max_buf = buf
        for c in range(2):
            m_prev = m_s[c]
            m_next = jnp.maximum(m_prev, jnp.max(max_buf[c], axis=1, keepdims=True))
            m_wide = pltpu.repeat(m_next, KEY_BLOCK // LANES, axis=1)
            alpha = jnp.exp2(m_prev - m_next)
            m_s[c] = m_next
            psum = None
            pv = None
            for kb in range(tk // KEY_BLOCK):
                off = pl.multiple_of(t * tk + kb * KEY_BLOCK, KEY_BLOCK)
                p = jnp.exp2(s_buf[c, :, kb * KEY_BLOCK:(kb + 1) * KEY_BLOCK] - m_wide)
                part = lane_fold(p, jnp.add)
                psum = part if psum is None else psum + part
                d = _dot(p.astype(BF16), v_ref[pl.ds(off, KEY_BLOCK), :])
                pv = d if pv is None else pv + d
            l_s[c] = alpha * l_s[c] + jnp.sum(psum, axis=1, keepdims=True)
            acc_s[c] = acc_s[c] * pltpu.repeat(alpha, vw // LANES, axis=1) + pv

    buf_a = (sa_s, ma_s)
    buf_b = (sb_s, mb_s)
    scores(nf, buf_a, True)

    def pair(n, carry):
        u = 2 * n
        scores(u, buf_b, False)
        absorb(tile_at(u), buf_a)
        scores(u + 1, buf_a, False)
        absorb(u, buf_b)
        return carry

    lax.fori_loop(0, nf // 2, pair, 0)

    @pl.when(nf % 2 == 0)
    def _():
        absorb(tile_at(nf), buf_a)

    @pl.when(nf % 2 == 1)
    def _():
        scores(nf - 1, buf_b, False)
        absorb(tile_at(nf - 1), buf_a)
        absorb(nf - 1, buf_b)

    o0 = acc_s[0] / pltpu.repeat(l_s[0], vw // LANES, axis=1)
    o1 = acc_s[1] / pltpu.repeat(l_s[1], vw // LANES, axis=1)
    o = o0 - _lam(lq1_ref, lk1_ref, lq2_ref, lk2_ref, lam_init) * o1
    o_ref[...] = _sub_norm(o, subln_ref, lam_init).astype(o_ref.dtype)


def _attn_prompt(q, k, v, lq1, lk1, lq2, lk2, subln, *, batch, seq, lam_init, tq, tk):
    n, width = q.shape
    vw = subln.shape[-1]
    heads = width // vw
    nq = seq // tq
    vec = lambda a: a.reshape(1, -1)
    small = lambda w: pl.BlockSpec((1, w), lambda b, h, i: (0, 0))
    return pl.pallas_call(
        functools.partial(_attn_prompt_kernel, lam_init=lam_init),
        grid=(batch, heads, nq),
        in_specs=[
            pl.BlockSpec((tq, vw), lambda b, h, i: (b * nq + i, h)),
            pl.BlockSpec((seq, vw), lambda b, h, i: (b, h)),
            pl.BlockSpec((seq, vw), lambda b, h, i: (b, h)),
            small(HEAD_DIM), small(HEAD_DIM), small(HEAD_DIM), small(HEAD_DIM), small(vw),
        ],
        out_specs=pl.BlockSpec((tq, vw), lambda b, h, i: (b * nq + i, h)),
        out_shape=jax.ShapeDtypeStruct((n, width), BF16),
        scratch_shapes=[
            pltpu.VMEM((2, tq, LANES), F32),
            pltpu.VMEM((2, tq, LANES), F32),
            pltpu.VMEM((2, tq, vw), F32),
            pltpu.VMEM((2, tq, tk), F32),
            pltpu.VMEM((2, tq, tk), F32),
            pltpu.VMEM((2, tq, LANES), F32),
            pltpu.VMEM((2, tq, LANES), F32),
        ],
        compiler_params=_params("parallel", "parallel", "arbitrary"),
    )(q, k, v, vec(lq1), vec(lk1), vec(lq2), vec(lk2), vec(subln))


def _attn_sample_kernel(q_ref, kn_ref, vn_ref, ck_ref, cv_ref, lq1_ref, lk1_ref, lq2_ref, lk2_ref,
                        subln_ref, o_ref, m_s, l_s, acc_s, *, lam_init, heads):
    ts = q_ref.shape[0]
    vw = subln_ref.shape[1]
    tp = ck_ref.shape[1] // (heads * vw // LANES)
    step = pl.program_id(1)

    @pl.when(step == 0)
    def _():
        m_s[...] = jnp.full(m_s.shape, NEG_BIG, F32)
        l_s[...] = jnp.zeros(l_s.shape, F32)
        acc_s[...] = jnp.zeros(acc_s.shape, F32)

    def absorb(h, k, v):
        q = q_ref[:, h * vw:(h + 1) * vw]
        s = jnp.concatenate([_dot_nt(q[:, :HEAD_DIM], k[:, :HEAD_DIM]),
                             _dot_nt(q[:, HEAD_DIM:], k[:, HEAD_DIM:])], axis=0)
        m_prev = m_s[h]
        m_next = jnp.maximum(m_prev, jnp.max(s, axis=1, keepdims=True))
        p = jnp.exp2(s - m_next[:, :1])
        alpha = jnp.exp2(m_prev - m_next)
        l_s[h] = alpha * l_s[h] + jnp.sum(p, axis=1, keepdims=True)
        m_s[h] = m_next
        acc_s[h] = acc_s[h] * pltpu.repeat(alpha, vw // LANES, axis=1) + _dot(p.astype(BF16), v)

    def cached(ref, h):
        halves = [ref[0, pl.ds(c * heads + h, tp, stride=2 * heads), :] for c in range(vw // LANES)]
        return jnp.concatenate(halves, axis=1).astype(BF16)

    for h in range(heads):
        absorb(h, cached(ck_ref, h), cached(cv_ref, h))

    @pl.when(step == pl.num_programs(1) - 1)
    def _():
        lam = _lam(lq1_ref, lk1_ref, lq2_ref, lk2_ref, lam_init)
        for h in range(heads):
            hs = slice(h * vw, (h + 1) * vw)
            absorb(h, kn_ref[:, hs], vn_ref[:, hs])
            o = acc_s[h] / pltpu.repeat(l_s[h], vw // LANES, axis=1)
            o = o[:ts] - lam * o[ts:]
            o_ref[:, hs] = _sub_norm(o, subln_ref, lam_init).astype(o_ref.dtype)


def _attn_sample(q, kn, vn, cache_k, cache_v, lq1, lk1, lq2, lk2, subln, *, ts, lam_init, tp):
    n, width = q.shape
    b, past, heads, vw = cache_k.shape
    nl = vw // LANES

    def by_lane_tile(c):
        c = c.reshape(b, past, heads, nl, LANES).transpose(0, 1, 3, 2, 4)
        return c.reshape(b, past * nl * heads, LANES)

    cache_k = by_lane_tile(cache_k)
    cache_v = by_lane_tile(cache_v)
    vec = lambda a: a.reshape(1, -1)
    small = lambda w: pl.BlockSpec((1, w), lambda i, p: (0, 0))
    new = pl.BlockSpec((ts, width), lambda i, p: (i, 0))
    old = pl.BlockSpec((1, tp * nl * heads, LANES), lambda i, p: (i, p, 0))
    return pl.pallas_call(
        functools.partial(_attn_sample_kernel, lam_init=lam_init, heads=heads),
        grid=(b, past // tp),
        in_specs=[new, new, new, old, old,
                  small(HEAD_DIM), small(HEAD_DIM), small(HEAD_DIM), small(HEAD_DIM), small(vw)],
        out_specs=new,
        out_shape=jax.ShapeDtypeStruct((n, width), BF16),
        scratch_shapes=[
            pltpu.VMEM((heads, 2 * ts, LANES), F32),
            pltpu.VMEM((heads, 2 * ts, LANES), F32),
            pltpu.VMEM((heads, 2 * ts, vw), F32),
        ],
        compiler_params=_params("parallel", "arbitrary"),
    )(q, kn, vn, cache_k, cache_v, vec(lq1), vec(lk1), vec(lq2), vec(lk2), vec(subln))


def _oproj_kernel(o_ref, w_ref, x_ref, out_ref):
    out_ref[...] = x_ref[...] + _dot(o_ref[...], w_ref[...])


def _oproj(o, w, x, *, tm, tn):
    n, d = x.shape
    width = o.shape[1]
    return pl.pallas_call(
        _oproj_kernel,
        grid=(n // tm, d // tn),
        in_specs=[
            pl.BlockSpec((tm, width), lambda i, j: (i, 0)),
            pl.BlockSpec((width, tn), lambda i, j: (0, j)),
            pl.BlockSpec((tm, tn), lambda i, j: (i, j)),
        ],
        out_specs=pl.BlockSpec((tm, tn), lambda i, j: (i, j)),
        out_shape=jax.ShapeDtypeStruct((n, d), F32),
        compiler_params=_params("parallel", "arbitrary"),
    )(o, w, x)


def _tiles(batch, seq):
    if seq >= 512:
        return dict(nb=1, tm=512)
    return dict(nb=batch, tm=seq)


def _trunk(x, pos0, pool_hist, ffn_hist, past_k, past_v, wts):
    b, seq, d = x.shape
    tl = _tiles(b, seq)
    rows = tl["nb"] * tl["tm"]
    heads_w = wts["w_k"].shape[1]
    vw = wts["subln"].shape[-1]
    heads = heads_w // vw
    dff = wts["w_gate"].shape[-1]
    tf = 512 if dff % 512 == 0 else LANES
    assert len(pool_hist) == 1 and wts["w_q"].shape[0] == 1 and vw == 2 * HEAD_DIM

    x1, pool_state = _pool_layer(x, pool_hist[0], wts["norm_mix"][0], wts["pool_w"][0],
                                 wts["pool_scale"][0], pos0=pos0, **tl)
    x2, ffn_state0 = _ffn_layer(x1, ffn_hist[0], wts["norm_ffn"][0], wts["w_gate"][0], wts["w_up"][0],
                                wts["conv_w"][0], wts["conv_b"][0], wts["w_down"][0],
                                wts["norm_final"], final=False, tf=tf, **tl)

    cos, sin = _rope_tables(seq, pos0)
    if rows > seq:
        cos = jnp.tile(cos, (rows // seq, 1))
        sin = jnp.tile(sin, (rows // seq, 1))
    x2f = x2.reshape(b * seq, d)
    k, v, kb, vb, qb = _qkv_proj(x2f, wts["norm_kv"], wts["norm_mix"][1], wts["w_k"], wts["w_v"],
                                 wts["w_q"][0], cos, sin, tm=rows, tn=512)
    lam_init = _lam_init(1)
    lam_args = (wts["lam_q1"][0], wts["lam_k1"][0], wts["lam_q2"][0], wts["lam_k2"][0], wts["subln"][0])
    if past_k is None:
        tk = min(1024, seq)
        assert seq % tk == 0 and tk % 512 == 0
        o = _attn_prompt(qb, kb, vb, *lam_args, batch=b, seq=seq, lam_init=lam_init, tq=512, tk=tk)
    else:
        past = past_k.shape[1]
        assert past % CHUNK == 0 and seq <= CHUNK
        tp = min(1024, past)
        assert past % tp == 0
        o = _attn_sample(qb, kb, vb, past_k, past_v, *lam_args, ts=seq, lam_init=lam_init, tp=tp)
    x3 = _oproj(o, wts["w_o"][0], x2f, tm=rows, tn=d).reshape(b, seq, d)
    y, ffn_state1 = _ffn_layer(x3, ffn_hist[1], wts["norm_ffn"][1], wts["w_gate"][1], wts["w_up"][1],
                               wts["conv_w"][1], wts["conv_b"][1], wts["w_down"][1],
                               wts["norm_final"], final=True, tf=tf, **tl)
    return (y, pool_state[None], jnp.stack([ffn_state0, ffn_state1]),
            k.reshape(b, seq, heads, vw), v.reshape(b, seq, heads, vw))


def kernel(x_prompt, x_sample, cache_pool, cache_ffn_conv, cache_k, cache_v, norm_mix, pool_w, pool_scale, norm_kv, w_k, w_v, w_q, lam_q1, lam_k1, lam_q2, lam_k2, subln, w_o, norm_ffn, w_gate, w_up, conv_w, conv_b, w_down, norm_final):
    wts = dict(norm_mix=norm_mix, pool_w=pool_w.astype(BF16), pool_scale=pool_scale, norm_kv=norm_kv,
               w_k=w_k.astype(BF16), w_v=w_v.astype(BF16), w_q=w_q.astype(BF16),
               lam_q1=lam_q1, lam_k1=lam_k1, lam_q2=lam_q2, lam_k2=lam_k2, subln=subln,
               w_o=w_o.astype(BF16), norm_ffn=norm_ffn, w_gate=w_gate.astype(BF16),
               w_up=w_up.astype(BF16), conv_w=conv_w, conv_b=conv_b, w_down=w_down.astype(BF16),
               norm_final=norm_final)
    bp, _, d = x_prompt.shape
    dff = w_gate.shape[-1]
    n_a = cache_pool.shape[0]
    depth = cache_ffn_conv.shape[0]
    pool_hist_p = jnp.zeros((n_a, bp, POOL_HIST, d), F32)
    ffn_hist_p = jnp.zeros((depth, bp, CONV_W - 1, dff), F32)
    y_p, pool_p, ffn_p, k_p, v_p = _trunk(x_prompt, 0, pool_hist_p, ffn_hist_p, None, None, wts)
    y_s, pool_s, ffn_s, k_s, v_s = _trunk(x_sample, cache_k.shape[1], cache_pool, cache_ffn_conv,
                                          cache_k, cache_v, wts)
    return (y_p, y_s, pool_p, pool_s, ffn_p, ffn_s, k_p, v_p, k_s, v_s)
```

```python
import functools
import math

import jax
import jax.numpy as jnp
from jax import lax
from jax.experimental import pallas as pl
from jax.experimental.pallas import tpu as pltpu

F32 = jnp.float32
BF16 = jnp.bfloat16

CHUNK = 64
POOL_WINDOWS = (2, 4, 8, 16)
POOL_HIST = max(POOL_WINDOWS) - 1
POOL_HALO = POOL_HIST + 1
CONV_W = 3
CONV_HALO = 8
HEAD_DIM = 128
ROPE_THETA = 10000.0
EPS = 1e-5
NEG_BIG = -1e30
QUERY_SCALE = HEAD_DIM ** -0.5 * math.log2(math.e)
LANES = 128
KEY_BLOCK = 256
NORM_ROWS = 256
VMEM_LIMIT = 56 * 1024 * 1024


def _params(*sem):
    return pltpu.CompilerParams(dimension_semantics=sem, vmem_limit_bytes=VMEM_LIMIT)


def _dot(a, b):
    return jnp.dot(a, b, preferred_element_type=F32)


def _dot_nt(a, b):
    return lax.dot_general(a, b, (((1,), (1,)), ((), ())), preferred_element_type=F32)


def _rms(x, gain):
    return x * lax.rsqrt(jnp.mean(x * x, axis=-1, keepdims=True) + EPS) * gain


def _lam_init(layer):
    return 0.8 - 0.6 * math.exp(-0.3 * layer)


def _rope_table_kernel(cos_ref, sin_ref, *, pos0):
    rows = cos_ref.shape[0]
    half = HEAD_DIM // 2
    row = lax.broadcasted_iota(jnp.int32, (rows, HEAD_DIM), 0) + pl.program_id(0) * rows
    lane = lax.broadcasted_iota(jnp.int32, (rows, HEAD_DIM), 1)
    idx = jnp.where(lane < half, lane, lane - half).astype(F32)
    inv = jnp.exp(-math.log(ROPE_THETA) * idx / half)
    ang = (row + pos0).astype(F32) * inv
    cos_ref[...] = jnp.cos(ang)
    sin = jnp.sin(ang)
    sin_ref[...] = jnp.where(lane < half, -sin, sin)


def _rope_tables(seq, pos0):
    rows = min(seq, 2048)
    assert seq % rows == 0
    return pl.pallas_call(
        functools.partial(_rope_table_kernel, pos0=pos0),
        grid=(seq // rows,),
        out_specs=[pl.BlockSpec((rows, HEAD_DIM), lambda i: (i, 0))] * 2,
        out_shape=[jax.ShapeDtypeStruct((seq, HEAD_DIM), F32)] * 2,
        compiler_params=_params("parallel"),
    )()


def _pool_kernel(x_ref, hist_ref, gain_ref, w_ref, scale_ref, out_ref, state_ref, ext_s, *, pos0):
    nb, tm, d = x_ref.shape
    group = d // len(POOL_WINDOWS)
    t = pl.program_id(1)

    @pl.when(t == 0)
    def _():
        ext_s[:, :POOL_HALO, :] = hist_ref[...]

    x = x_ref[...]
    ext_s[:, POOL_HALO:, :] = _rms(x, gain_ref[...])
    pos = pos0 + t * tm + lax.broadcasted_iota(jnp.int32, (1, tm, 1), 1)
    for g, w in enumerate(POOL_WINDOWS):
        sl = slice(g * group, (g + 1) * group)
        ext = ext_s[:, :, sl]
        s = ext
        k = 1
        while k < w:
            s = s + pltpu.roll(s, k, axis=1)
            k *= 2
        inv_cnt = 1.0 / jnp.minimum(pos + 1, w).astype(F32)
        dd = s[:, POOL_HALO:, :] * inv_cnt - ext[:, POOL_HALO:, :]
        mix = _dot(dd.reshape(nb * tm, group).astype(BF16), w_ref[g]).reshape(nb, tm, group)
        out_ref[:, :, sl] = x[:, :, sl] + mix * scale_ref[:, sl]
    state_ref[...] = ext_s[:, tm + 1:, :]
    ext_s[:, :POOL_HALO, :] = ext_s[:, tm:, :]


def _pool_layer(x, hist, gain, w, scale, *, pos0, nb, tm):
    b, seq, d = x.shape
    ng = len(POOL_WINDOWS)
    group = d // ng
    hist = jnp.pad(hist, ((0, 0), (POOL_HALO - POOL_HIST, 0), (0, 0)))
    return pl.pallas_call(
        functools.partial(_pool_kernel, pos0=pos0),
        grid=(b // nb, seq // tm),
        in_specs=[
            pl.BlockSpec((nb, tm, d), lambda i, t: (i, t, 0)),
            pl.BlockSpec((nb, POOL_HALO, d), lambda i, t: (i, 0, 0)),
            pl.BlockSpec((1, d), lambda i, t: (0, 0)),
            pl.BlockSpec((ng, group, group), lambda i, t: (0, 0, 0)),
            pl.BlockSpec((1, d), lambda i, t: (0, 0)),
        ],
        out_specs=[
            pl.BlockSpec((nb, tm, d), lambda i, t: (i, t, 0)),
            pl.BlockSpec((nb, POOL_HIST, d), lambda i, t: (i, 0, 0)),
        ],
        out_shape=[
            jax.ShapeDtypeStruct((b, seq, d), F32),
            jax.ShapeDtypeStruct((b, POOL_HIST, d), F32),
        ],
        scratch_shapes=[pltpu.VMEM((nb, POOL_HALO + tm, d), F32)],
        compiler_params=_params("parallel", "arbitrary"),
    )(x, hist, gain.reshape(1, d), w, scale.reshape(1, d))


def _ffn_kernel(x_ref, hist_ref, gain_ref, wg_ref, wu_ref, cw_ref, cb_ref, wd_ref, gfin_ref,
                out_ref, state_ref, h_s, tail_s, act_s, *, final, nf):
    nb, tm, d = x_ref.shape
    tf = wg_ref.shape[1]
    t = pl.program_id(1)
    j = pl.program_id(2)

    def activate(slot):
        @pl.when(t == 0)
        def _():
            tail_s[j] = hist_ref[...]

        h = h_s[...]
        gate = _dot(h, wg_ref[...]).reshape(nb, tm, tf)
        up = _dot(h, wu_ref[...]).reshape(nb, tm, tf)
        ext = jnp.concatenate([tail_s[j], gate], axis=1)
        prev1 = pltpu.roll(ext, 1, axis=1)[:, CONV_HALO:, :]
        prev2 = pltpu.roll(ext, 2, axis=1)[:, CONV_HALO:, :]
        cw = cw_ref[...]
        c = cb_ref[...] + prev2 * cw[0:1] + prev1 * cw[1:2] + gate * cw[2:3]
        act = c * (1.0 / (1.0 + jnp.exp(-c))) * up
        tail_s[j] = gate[:, tm - CONV_HALO:, :]
        state_ref[...] = gate[:, None, tm - (CONV_W - 1):, :]
        act_s[slot] = act.reshape(nb * tm, tf).astype(BF16)

    def project(slot):
        out_ref[...] += _dot(act_s[slot], wd_ref[...]).reshape(nb, tm, d)

    @pl.when(j == 0)
    def _():
        h_s[...] = _rms(x_ref[...], gain_ref[...]).reshape(nb * tm, d).astype(BF16)
        out_ref[...] = x_ref[...]
        activate(0)

    for slot in range(2):
        @pl.when((j > 0) & (j < nf) & (j % 2 == slot))
        def _():
            activate(slot)
            project(1 - slot)

    @pl.when(j == nf)
    def _():
        project((nf - 1) % 2)
        if final:
            out_ref[...] = _rms(out_ref[...], gfin_ref[...])


def _ffn_layer(x, hist, gain, wg, wu, cw, cb, wd, gfin, *, final, nb, tm, tf):
    b, seq, d = x.shape
    dff = wg.shape[1]
    nf = dff // tf
    hist = jnp.pad(hist, ((0, 0), (CONV_HALO - (CONV_W - 1), 0), (0, 0)))
    up = lambda j: jnp.minimum(j, nf - 1)
    dn = lambda j: jnp.maximum(j - 1, 0)
    out, tails = pl.pallas_call(
        functools.partial(_ffn_kernel, final=final, nf=nf),
        grid=(b // nb, seq // tm, nf + 1),
        in_specs=[
            pl.BlockSpec((nb, tm, d), lambda i, t, j: (i, t, 0)),
            pl.BlockSpec((nb, CONV_HALO, tf), lambda i, t, j: (i, 0, up(j))),
            pl.BlockSpec((1, d), lambda i, t, j: (0, 0)),
            pl.BlockSpec((d, tf), lambda i, t, j: (0, up(j))),
            pl.BlockSpec((d, tf), lambda i, t, j: (0, up(j))),
            pl.BlockSpec((CONV_W, tf), lambda i, t, j: (0, up(j))),
            pl.BlockSpec((1, tf), lambda i, t, j: (0, up(j))),
            pl.BlockSpec((tf, d), lambda i, t, j: (dn(j), 0)),
            pl.BlockSpec((1, d), lambda i, t, j: (0, 0)),
        ],
        out_specs=[
            pl.BlockSpec((nb, tm, d), lambda i, t, j: (i, t, 0)),
            pl.BlockSpec((nb, 1, CONV_W - 1, tf), lambda i, t, j: (i, t, 0, up(j))),
        ],
        out_shape=[
            jax.ShapeDtypeStruct((b, seq, d), F32),
            jax.ShapeDtypeStruct((b, seq // tm, CONV_W - 1, dff), F32),
        ],
        scratch_shapes=[
            pltpu.VMEM((nb * tm, d), BF16),
            pltpu.VMEM((nf, nb, CONV_HALO, tf), F32),
            pltpu.VMEM((2, nb * tm, tf), BF16),
        ],
        compiler_params=_params("parallel", "arbitrary", "arbitrary"),
    )(x, hist, gain.reshape(1, d), wg, wu, cw, cb.reshape(1, dff), wd, gfin.reshape(1, d))
    return out, tails[:, -1]


def _qkv_kernel(x_ref, gkv_ref, gq_ref, wk_ref, wv_ref, wq_ref, cos_ref, sin_ref,
                k_ref, v_ref, kb_ref, vb_ref, qb_ref, hk_s, hq_s):
    tn = wk_ref.shape[1]

    @pl.when(pl.program_id(1) == 0)
    def _():
        rows = min(NORM_ROWS, x_ref.shape[0])
        for r in range(0, x_ref.shape[0], rows):
            x = x_ref[r:r + rows, :]
            xn = x * lax.rsqrt(jnp.mean(x * x, axis=-1, keepdims=True) + EPS)
            hk_s[r:r + rows, :] = (xn * gkv_ref[...]).astype(BF16)
            hq_s[r:r + rows, :] = (xn * gq_ref[...]).astype(BF16)

    cos = cos_ref[...]
    sin = sin_ref[...]
    hk = hk_s[...]
    k = _dot(hk, wk_ref[...])
    q = _dot(hq_s[...], wq_ref[...])
    v = _dot(hk, wv_ref[...])
    v_ref[...] = v
    vb_ref[...] = v.astype(BF16)
    for c in range(tn // HEAD_DIM):
        sl = slice(c * HEAD_DIM, (c + 1) * HEAD_DIM)
        kc = k[:, sl]
        kc = kc * cos + pltpu.roll(kc, HEAD_DIM // 2, axis=1) * sin
        k_ref[:, sl] = kc
        kb_ref[:, sl] = kc.astype(BF16)
        qc = q[:, sl]
        qc = qc * cos + pltpu.roll(qc, HEAD_DIM // 2, axis=1) * sin
        qb_ref[:, sl] = (qc * QUERY_SCALE).astype(BF16)


def _qkv_proj(x, gkv, gq, wk, wv, wq, cos, sin, *, tm, tn):
    n, d = x.shape
    width = wk.shape[1]
    ncos = cos.shape[0] // tm
    row = lambda i, j: (i, 0)
    col = lambda i, j: (0, j)
    tile = lambda i, j: (i, j)
    return pl.pallas_call(
        _qkv_kernel,
        grid=(n // tm, width // tn),
        in_specs=[
            pl.BlockSpec((tm, d), row),
            pl.BlockSpec((1, d), lambda i, j: (0, 0)),
            pl.BlockSpec((1, d), lambda i, j: (0, 0)),
            pl.BlockSpec((d, tn), col),
            pl.BlockSpec((d, tn), col),
            pl.BlockSpec((d, tn), col),
            pl.BlockSpec((tm, HEAD_DIM), lambda i, j: (i % ncos, 0)),
            pl.BlockSpec((tm, HEAD_DIM), lambda i, j: (i % ncos, 0)),
        ],
        out_specs=[pl.BlockSpec((tm, tn), tile)] * 5,
        out_shape=[
            jax.ShapeDtypeStruct((n, width), F32),
            jax.ShapeDtypeStruct((n, width), F32),
            jax.ShapeDtypeStruct((n, width), BF16),
            jax.ShapeDtypeStruct((n, width), BF16),
            jax.ShapeDtypeStruct((n, width), BF16),
        ],
        scratch_shapes=[pltpu.VMEM((tm, d), BF16), pltpu.VMEM((tm, d), BF16)],
        compiler_params=_params("parallel", "arbitrary"),
    )(x, gkv.reshape(1, d), gq.reshape(1, d), wk, wv, wq, cos, sin)


def _lam(lq1_ref, lk1_ref, lq2_ref, lk2_ref, lam_init):
    a = jnp.exp(jnp.sum(lq1_ref[...] * lk1_ref[...], axis=-1, keepdims=True))
    b = jnp.exp(jnp.sum(lq2_ref[...] * lk2_ref[...], axis=-1, keepdims=True))
    return a - b + lam_init


def _sub_norm(o, subln_ref, lam_init):
    return _rms(o, subln_ref[...]) * (1.0 - lam_init)


def _attn_prompt_kernel(q_ref, k_ref, v_ref, lq1_ref, lk1_ref, lq2_ref, lk2_ref, subln_ref,
                        o_ref, m_s, l_s, acc_s, sa_s, sb_s, ma_s, mb_s, *, lam_init):
    tq = q_ref.shape[0]
    tk = sa_s.shape[2]
    vw = v_ref.shape[1]
    i = pl.program_id(2)
    nf = i // (tk // tq)
    diag_shift = (i % (tk // tq)) * (tq // CHUNK)

    m_s[...] = jnp.full(m_s.shape, NEG_BIG, F32)
    l_s[...] = jnp.zeros(l_s.shape, F32)
    acc_s[...] = jnp.zeros(acc_s.shape, F32)

    def tile_at(u):
        return jnp.where(u == 0, nf, u - 1)

    def lane_fold(x, op):
        out = x[:, :LANES]
        for g in range(1, x.shape[1] // LANES):
            out = op(out, x[:, g * LANES:(g + 1) * LANES])
        return out

    def scores(t, buf, masked):
        s_buf, max_buf = buf
        for c in range(2):
            sl = slice(c * HEAD_DIM, (c + 1) * HEAD_DIM)
            q = q_ref[:, sl]
            fold = None
            for kb in range(tk // KEY_BLOCK):
                off = pl.multiple_of(t * tk + kb * KEY_BLOCK, KEY_BLOCK)
                s = _dot_nt(q, k_ref[pl.ds(off, KEY_BLOCK), sl])
                if masked:
                    shape = (tq, KEY_BLOCK)
                    qc = lax.broadcasted_iota(jnp.int32, shape, 0) // CHUNK + diag_shift
                    kc = lax.broadcasted_iota(jnp.int32, shape, 1) // CHUNK + kb * (KEY_BLOCK // CHUNK)
                    s = jnp.where(kc <= qc, s, NEG_BIG)
                s_buf[c, :, kb * KEY_BLOCK:(kb + 1) * KEY_BLOCK] = s
                part = lane_fold(s, jnp.maximum)
                fold = part if fold is None else jnp.maximum(fold, part)
            max_buf[c] = fold

    def absorb(t, buf):
        s_buf, max_buf = buf
        for c in range(2):
            m_prev = m_s[c]
            m_next = jnp.maximum(m_prev, jnp.max(max_buf[c], axis=1, keepdims=True))
            m_wide = pltpu.repeat(m_next, KEY_BLOCK // LANES, axis=1)
            alpha = jnp.exp2(m_prev - m_next)
            m_s[c] = m_next
            psum = None
            pv = None
            for kb in range(tk // KEY_BLOCK):
                off = pl.multiple_of(t * tk + kb * KEY_BLOCK, KEY_BLOCK)
                p = jnp.exp2(s_buf[c, :, kb * KEY_BLOCK:(kb + 1) * KEY_BLOCK] - m_wide)
                part = lane_fold(p, jnp.add)
                psum = part if psum is None else psum + part
                d = _dot(p.astype(BF16), v_ref[pl.ds(off, KEY_BLOCK), :])
                pv = d if pv is None else pv + d
            l_s[c] = alpha * l_s[c] + jnp.sum(psum, axis=1, keepdims=True)
            acc_s[c] = acc_s[c] * pltpu.repeat(alpha, vw // LANES, axis=1) + pv

    buf_a = (sa_s, ma_s)
    buf_b = (sb_s, mb_s)
    scores(nf, buf_a, True)

    def pair(n, carry):
        u = 2 * n
        scores(u, buf_b, False)
        absorb(tile_at(u), buf_a)
        scores(u + 1, buf_a, False)
        absorb(u, buf_b)
        return carry

    lax.fori_loop(0, nf // 2, pair, 0)

    @pl.when(nf % 2 == 0)
    def _():
        absorb(tile_at(nf), buf_a)

    @pl.when(nf % 2 == 1)
    def _():
        scores(nf - 1, buf_b, False)
        absorb(tile_at(nf - 1), buf_a)
        absorb(nf - 1, buf_b)

    o0 = acc_s[0] / pltpu.repeat(l_s[0], vw // LANES, axis=1)
    o1 = acc_s[1] / pltpu.repeat(l_s[1], vw // LANES, axis=1)
    o = o0 - _lam(lq1_ref, lk1_ref, lq2_ref, lk2_ref, lam_init) * o1
    o_ref[...] = _sub_norm(o, subln_ref, lam_init).astype(o_ref.dtype)


def _attn_prompt(q, k, v, lq1, lk1, lq2, lk2, subln, *, batch, seq, lam_init, tq, tk):
    n, width = q.shape
    vw = subln.shape[-1]
    heads = width // vw
    nq = seq // tq
    vec = lambda a: a.reshape(1, -1)
    small = lambda w: pl.BlockSpec((1, w), lambda b, h, i: (0, 0))
    return pl.pallas_call(
        functools.partial(_attn_prompt_kernel, lam_init=lam_init),
        grid=(batch, heads, nq),
        in_specs=[
            pl.BlockSpec((tq, vw), lambda b, h, i: (b * nq + i, h)),
            pl.BlockSpec((seq, vw), lambda b, h, i: (b, h)),
            pl.BlockSpec((seq, vw), lambda b, h, i: (b, h)),
            small(HEAD_DIM), small(HEAD_DIM), small(HEAD_DIM), small(HEAD_DIM), small(vw),
        ],
        out_specs=pl.BlockSpec((tq, vw), lambda b, h, i: (b * nq + i, h)),
        out_shape=jax.ShapeDtypeStruct((n, width), BF16),
        scratch_shapes=[
            pltpu.VMEM((2, tq, LANES), F32),
            pltpu.VMEM((2, tq, LANES), F32),
            pltpu.VMEM((2, tq, vw), F32),
            pltpu.VMEM((2, tq, tk), F32),
            pltpu.VMEM((2, tq, tk), F32),
            pltpu.VMEM((2, tq, LANES), F32),
            pltpu.VMEM((2, tq, LANES), F32),
        ],
        compiler_params=_params("parallel", "parallel", "arbitrary"),
    )(q, k, v, vec(lq1), vec(lk1), vec(lq2), vec(lk2), vec(subln))


def _attn_sample_kernel(q_ref, kn_ref, vn_ref, ck_ref, cv_ref, lq1_ref, lk1_ref, lq2_ref, lk2_ref,
                        subln_ref, o_ref, m_s, l_s, acc_s, *, lam_init, heads):
    ts = q_ref.shape[0]
    vw = subln_ref.shape[1]
    tp = ck_ref.shape[1] // (heads * vw // LANES)
    step = pl.program_id(1)

    @pl.when(step == 0)
    def _():
        m_s[...] = jnp.full(m_s.shape, NEG_BIG, F32)
        l_s[...] = jnp.zeros(l_s.shape, F32)
        acc_s[...] = jnp.zeros(acc_s.shape, F32)

    def absorb(h, k, v):
        q = q_ref[:, h * vw:(h + 1) * vw]
        s = jnp.concatenate([_dot_nt(q[:, :HEAD_DIM], k[:, :HEAD_DIM]),
                             _dot_nt(q[:, HEAD_DIM:], k[:, HEAD_DIM:])], axis=0)
        m_prev = m_s[h]
        m_next = jnp.maximum(m_prev, jnp.max(s, axis=1, keepdims=True))
        p = jnp.exp2(s - m_next[:, :1])
        alpha = jnp.exp2(m_prev - m_next)
        l_s[h] = alpha * l_s[h] + jnp.sum(p, axis=1, keepdims=True)
        m_s[h] = m_next
        acc_s[h] = acc_s[h] * pltpu.repeat(alpha, vw // LANES, axis=1) + _dot(p.astype(BF16), v)

    def cached(ref, h):
        halves = [ref[0, pl.ds(c * heads + h, tp, stride=2 * heads), :] for c in range(vw // LANES)]
        return jnp.concatenate(halves, axis=1).astype(BF16)

    for h in range(heads):
        absorb(h, cached(ck_ref, h), cached(cv_ref, h))

    @pl.when(step == pl.num_programs(1) - 1)
    def _():
        lam = _lam(lq1_ref, lk1_ref, lq2_ref, lk2_ref, lam_init)
        for h in range(heads):
            hs = slice(h * vw, (h + 1) * vw)
            absorb(h, kn_ref[:, hs], vn_ref[:, hs])
            o = acc_s[h] / pltpu.repeat(l_s[h], vw // LANES, axis=1)
            o = o[:ts] - lam * o[ts:]
            o_ref[:, hs] = _sub_norm(o, subln_ref, lam_init).astype(o_ref.dtype)


def _attn_sample(q, kn, vn, cache_k, cache_v, lq1, lk1, lq2, lk2, subln, *, ts, lam_init, tp):
    n, width = q.shape
    b, past, heads, vw = cache_k.shape
    nl = vw // LANES

    def by_lane_tile(c):
        c = c.reshape(b, past, heads, nl, LANES).transpose(0, 1, 3, 2, 4)
        return c.reshape(b, past * nl * heads, LANES)

    cache_k = by_lane_tile(cache_k)
    cache_v = by_lane_tile(cache_v)
    vec = lambda a: a.reshape(1, -1)
    small = lambda w: pl.BlockSpec((1, w), lambda i, p: (0, 0))
    new = pl.BlockSpec((ts, width), lambda i, p: (i, 0))
    old = pl.BlockSpec((1, tp * nl * heads, LANES), lambda i, p: (i, p, 0))
    return pl.pallas_call(
        functools.partial(_attn_sample_kernel, lam_init=lam_init, heads=heads),
        grid=(b, past // tp),
        in_specs=[new, new, new, old, old,
                  small(HEAD_DIM), small(HEAD_DIM), small(HEAD_DIM), small(HEAD_DIM), small(vw)],
        out_specs=new,
        out_shape=jax.ShapeDtypeStruct((n, width), BF16),
        scratch_shapes=[
            pltpu.VMEM((heads, 2 * ts, LANES), F32),
            pltpu.VMEM((heads, 2 * ts, LANES), F32),
            pltpu.VMEM((heads, 2 * ts, vw), F32),
        ],
        compiler_params=_params("parallel", "arbitrary"),
    )(q, kn, vn, cache_k, cache_v, vec(lq1), vec(lk1), vec(lq2), vec(lk2), vec(subln))


def _oproj_kernel(o_ref, w_ref, x_ref, out_ref):
    out_ref[...] = x_ref[...] + _dot(o_ref[...], w_ref[...])


def _oproj(o, w, x, *, tm, tn):
    n, d = x.shape
    width = o.shape[1]
    return pl.pallas_call(
        _oproj_kernel,
        grid=(n // tm, d // tn),
        in_specs=[
            pl.BlockSpec((tm, width), lambda i, j: (i, 0)),
            pl.BlockSpec((width, tn), lambda i, j: (0, j)),
            pl.BlockSpec((tm, tn), lambda i, j: (i, j)),
        ],
        out_specs=pl.BlockSpec((tm, tn), lambda i, j: (i, j)),
        out_shape=jax.ShapeDtypeStruct((n, d), F32),
        compiler_params=_params("parallel", "arbitrary"),
    )(o, w, x)


def _tiles(batch, seq):
    if seq >= 512:
        return dict(nb=1, tm=512)
    return dict(nb=batch, tm=seq)


def _qkv_tiles(seq, rows):
    if seq % 1024 == 0:
        return dict(tm=1024, tn=256)
    return dict(tm=rows, tn=512)


def _trunk(x, pos0, pool_hist, ffn_hist, past_k, past_v, wts):
    b, seq, d = x.shape
    tl = _tiles(b, seq)
    rows = tl["nb"] * tl["tm"]
    heads_w = wts["w_k"].shape[1]
    vw = wts["subln"].shape[-1]
    heads = heads_w // vw
    dff = wts["w_gate"].shape[-1]
    tf = 512 if dff % 512 == 0 else LANES
    assert len(pool_hist) == 1 and wts["w_q"].shape[0] == 1 and vw == 2 * HEAD_DIM

    x1, pool_state = _pool_layer(x, pool_hist[0], wts["norm_mix"][0], wts["pool_w"][0],
                                 wts["pool_scale"][0], pos0=pos0, **tl)
    x2, ffn_state0 = _ffn_layer(x1, ffn_hist[0], wts["norm_ffn"][0], wts["w_gate"][0], wts["w_up"][0],
                                wts["conv_w"][0], wts["conv_b"][0], wts["w_down"][0],
                                wts["norm_final"], final=False, tf=tf, **tl)

    cos, sin = _rope_tables(seq, pos0)
    if rows > seq:
        cos = jnp.tile(cos, (rows // seq, 1))
        sin = jnp.tile(sin, (rows // seq, 1))
    x2f = x2.reshape(b * seq, d)
    k, v, kb, vb, qb = _qkv_proj(x2f, wts["norm_kv"], wts["norm_mix"][1], wts["w_k"], wts["w_v"],
                                 wts["w_q"][0], cos, sin, **_qkv_tiles(seq, rows))
    lam_init = _lam_init(1)
    lam_args = (wts["lam_q1"][0], wts["lam_k1"][0], wts["lam_q2"][0], wts["lam_k2"][0], wts["subln"][0])
    if past_k is None:
        tk = min(1024, seq)
        assert seq % tk == 0 and tk % 512 == 0
        o = _attn_prompt(qb, kb, vb, *lam_args, batch=b, seq=seq, lam_init=lam_init, tq=512, tk=tk)
    else:
        past = past_k.shape[1]
        assert past % CHUNK == 0 and seq <= CHUNK
        tp = min(1024, past)
        assert past % tp == 0
        o = _attn_sample(qb, kb, vb, past_k, past_v, *lam_args, ts=seq, lam_init=lam_init, tp=tp)
    x3 = _oproj(o, wts["w_o"][0], x2f, tm=rows, tn=d).reshape(b, seq, d)
    y, ffn_state1 = _ffn_layer(x3, ffn_hist[1], wts["norm_ffn"][1], wts["w_gate"][1], wts["w_up"][1],
                               wts["conv_w"][1], wts["conv_b"][1], wts["w_down"][1],
                               wts["norm_final"], final=True, tf=tf, **tl)
    return (y, pool_state[None], jnp.stack([ffn_state0, ffn_state1]),
            k.reshape(b, seq, heads, vw), v.reshape(b, seq, heads, vw))


def kernel(x_prompt, x_sample, cache_pool, cache_ffn_conv, cache_k, cache_v, norm_mix, pool_w, pool_scale, norm_kv, w_k, w_v, w_q, lam_q1, lam_k1, lam_q2, lam_k2, subln, w_o, norm_ffn, w_gate, w_up, conv_w, conv_b, w_down, norm_final):
    wts = dict(norm_mix=norm_mix, pool_w=pool_w.astype(BF16), pool_scale=pool_scale, norm_kv=norm_kv,
               w_k=w_k.astype(BF16), w_v=w_v.astype(BF16), w_q=w_q.astype(BF16),
               lam_q1=lam_q1, lam_k1=lam_k1, lam_q2=lam_q2, lam_k2=lam_k2, subln=subln,
               w_o=w_o.astype(BF16), norm_ffn=norm_ffn, w_gate=w_gate.astype(BF16),
               w_up=w_up.astype(BF16), conv_w=conv_w, conv_b=conv_b, w_down=w_down.astype(BF16),
               norm_final=norm_final)
    bp, _, d = x_prompt.shape
    dff = w_gate.shape[-1]
    n_a = cache_pool.shape[0]
    depth = cache_ffn_conv.shape[0]
    pool_hist_p = jnp.zeros((n_a, bp, POOL_HIST, d), F32)
    ffn_hist_p = jnp.zeros((depth, bp, CONV_W - 1, dff), F32)
    y_p, pool_p, ffn_p, k_p, v_p = _trunk(x_prompt, 0, pool_hist_p, ffn_hist_p, None, None, wts)
    y_s, pool_s, ffn_s, k_s, v_s = _trunk(x_sample, cache_k.shape[1], cache_pool, cache_ffn_conv,
                                          cache_k, cache_v, wts)
    return (y_p, y_s, pool_p, pool_s, ffn_p, ffn_s, k_p, v_p, k_s, v_s)
```
